```python
import math
import jax, jax.numpy as jnp
from jax import lax
import numpy as np

D_MODEL = 1024
BATCH = 8
SEQ = 4096
DEPTH = 1

CHUNK = 64
EPS = 1e-6

SSM_WIDTH = 512
SSM_GROUP = 16
SSM_GROUPS = SSM_WIDTH // SSM_GROUP
SSM_STATE = 64
DT_MIN = 1e-3
DT_MAX = 1e-1

ATT_HEADS = 8
HEAD_DIM = 128
ATT_WIDTH = ATT_HEADS * HEAD_DIM
Q_BLOCK = 128

N_BRANCHES = 2
IN_SPLITS = (
    SSM_WIDTH,
    SSM_WIDTH + ATT_WIDTH,
    SSM_WIDTH + 2 * ATT_WIDTH,
    SSM_WIDTH + 3 * ATT_WIDTH,
    SSM_WIDTH + 3 * ATT_WIDTH + ATT_HEADS,
    SSM_WIDTH + 3 * ATT_WIDTH + ATT_HEADS + D_MODEL,
)
IN_COLS = SSM_WIDTH + 3 * ATT_WIDTH + ATT_HEADS + N_BRANCHES * D_MODEL

PEER_HEADS = 8
PEER_KEYS = 128
PEER_EXPERTS = PEER_KEYS * PEER_KEYS
PEER_QUERY = 256
PEER_HALF = PEER_QUERY // 2
PEER_TOPK = 16
TOKEN_BLOCK = 128

kernel_name = "hybrid_s5_fox_peer_block"


def rms_norm(x, g):
    x32 = x.astype(jnp.float32)
    y = x32 * lax.rsqrt(jnp.mean(x32 * x32, axis=-1, keepdims=True) + EPS)
    return (y * g.astype(jnp.float32)).astype(x.dtype)


def _complex_affine_combine(earlier, later):
    ar1, ai1, br1, bi1 = earlier
    ar2, ai2, br2, bi2 = later
    return (ar2 * ar1 - ai2 * ai1,
            ar2 * ai1 + ai2 * ar1,
            ar2 * br1 - ai2 * bi1 + br2,
            ar2 * bi1 + ai2 * br1 + bi2)


def s5_branch(u, a_re, a_im, log_step, b_re, b_im, c_re, c_im, d_skip, w_glu):
    bsz, seq = u.shape[0], u.shape[1]
    f32 = jnp.float32
    u32 = u.astype(f32)
    ug = u32.reshape(bsz, seq, SSM_GROUPS, SSM_GROUP)
    step = jnp.exp(log_step.astype(f32))[:, None]
    ar = a_re.astype(f32)
    ai = a_im.astype(f32)
    mag = jnp.exp(ar * step)
    lam_re = mag * jnp.cos(ai * step)
    lam_im = mag * jnp.sin(ai * step)
    nr = lam_re - 1.0
    ni = lam_im
    den = ar * ar + ai * ai
    coef_re = (nr * ar + ni * ai) / den
    coef_im = (ni * ar - nr * ai) / den
    br = coef_re[..., None] * b_re.astype(f32) - coef_im[..., None] * b_im.astype(f32)
    bi = coef_re[..., None] * b_im.astype(f32) + coef_im[..., None] * b_re.astype(f32)
    xr = jnp.einsum('blgh,gph->blgp', ug, br)
    xi = jnp.einsum('blgh,gph->blgp', ug, bi)
    lr = jnp.broadcast_to(lam_re[None, None], (1, seq, SSM_GROUPS, SSM_STATE))
    li = jnp.broadcast_to(lam_im[None, None], (1, seq, SSM_GROUPS, SSM_STATE))
    _, _, sr, si = lax.associative_scan(_complex_affine_combine, (lr, li, xr, xi), axis=1)
    y = (jnp.einsum('blgp,ghp->blgh', sr, c_re.astype(f32))
         - jnp.einsum('blgp,ghp->blgh', si, c_im.astype(f32)))
    y = y.reshape(bsz, seq, SSM_WIDTH) + d_skip.astype(f32) * u32
    y = jax.nn.gelu(y).astype(u.dtype)
    z = y @ w_glu
    val, gate = jnp.split(z, 2, axis=-1)
    return val * jax.nn.sigmoid(gate)


def head_rms_norm(x, g):
    x32 = x.astype(jnp.float32)
    y = x32 * lax.rsqrt(jnp.mean(x32 * x32, axis=-1, keepdims=True) + EPS)
    return (y * g.astype(jnp.float32)).astype(x.dtype)


def forgetting_attention(q, k, v, f_logit, f_bias, q_g, k_g):
    bsz, seq = q.shape[0], q.shape[1]
    q = head_rms_norm(q.reshape(bsz, seq, ATT_HEADS, HEAD_DIM), q_g).transpose(0, 2, 1, 3)
    k = head_rms_norm(k.reshape(bsz, seq, ATT_HEADS, HEAD_DIM), k_g).transpose(0, 2, 1, 3)
    v = v.reshape(bsz, seq, ATT_HEADS, HEAD_DIM).transpose(0, 2, 1, 3)
    log_f = jax.nn.log_sigmoid((f_logit + f_bias).astype(jnp.float32)).transpose(0, 2, 1)
    cum = jnp.cumsum(log_f, axis=-1)
    n_blk = seq // Q_BLOCK
    q_blocks = q.reshape(bsz, ATT_HEADS, n_blk, Q_BLOCK, HEAD_DIM).transpose(2, 0, 1, 3, 4)
    c_blocks = cum.reshape(bsz, ATT_HEADS, n_blk, Q_BLOCK).transpose(2, 0, 1, 3)
    key_pos = jnp.arange(seq)
    scale = HEAD_DIM ** -0.5

    def one_block(args):
        blk, q_blk, c_blk = args
        s = jnp.einsum('bhqd,bhkd->bhqk', q_blk, k).astype(jnp.float32) * scale
        s = s + c_blk[..., None] - cum[:, :, None, :]
        q_pos = blk * Q_BLOCK + jnp.arange(Q_BLOCK)
        s = jnp.where(key_pos[None, :] <= q_pos[:, None], s, -jnp.inf)
        p = jax.nn.softmax(s, axis=-1)
        return jnp.einsum('bhqk,bhkd->bhqd', p.astype(v.dtype), v)

    o = lax.map(one_block, (jnp.arange(n_blk), q_blocks, c_blocks))
    return o.transpose(1, 0, 3, 2, 4).reshape(bsz, seq, ATT_WIDTH)


def peer_ffn(h, w_query, sub_keys, u_tab, v_tab):
    bsz, seq, d = h.shape
    q = (h @ w_query).reshape(bsz, seq, PEER_HEADS, 2, PEER_HALF).astype(jnp.float32)
    scores = jnp.einsum('blhcd,hckd->blhck', q, sub_keys.astype(jnp.float32))
    s_top, i_top = lax.top_k(scores, PEER_TOPK)
    cand_s = (s_top[..., 0, :, None] + s_top[..., 1, None, :]).reshape(bsz, seq, PEER_HEADS, PEER_TOPK * PEER_TOPK)
    cand_i = (i_top[..., 0, :, None] * PEER_KEYS + i_top[..., 1, None, :]).reshape(bsz, seq, PEER_HEADS, PEER_TOPK * PEER_TOPK)
    best_s, pos = lax.top_k(cand_s, PEER_TOPK)
    expert_idx = jnp.take_along_axis(cand_i, pos, axis=-1)
    gates = jax.nn.softmax(best_s, axis=-1)
    n_tok = bsz * seq
    n_tb = n_tok // TOKEN_BLOCK
    h_blocks = h.reshape(n_tb, TOKEN_BLOCK, d)
    i_blocks = expert_idx.reshape(n_tb, TOKEN_BLOCK, PEER_HEADS, PEER_TOPK)
    g_blocks = gates.reshape(n_tb, TOKEN_BLOCK, PEER_HEADS, PEER_TOPK)

    def one_block(args):
        h_blk, i_blk, g_blk = args
        u_sel = u_tab[i_blk]
        act = jax.nn.gelu(jnp.einsum('td,thkd->thk', h_blk, u_sel).astype(jnp.float32))
        w = (g_blk * act).astype(h.dtype)
        return jnp.einsum('thk,thkd->td', w, v_tab[i_blk])

    out = lax.map(one_block, (h_blocks, i_blocks, g_blocks))
    return out.reshape(bsz, seq, d)


def setup_inputs(seed: int = 0) -> dict:
    key = jax.random.key(seed)
    ks = jax.random.split(key, 21)
    f32 = jnp.float32
    nrm = lambda k, shape: jax.random.normal(k, shape, dtype=f32)
    x = nrm(ks[0], (BATCH, SEQ, D_MODEL))
    mix_norm_g = 1.0 + 0.02 * nrm(ks[1], (DEPTH, D_MODEL))
    w_in = nrm(ks[2], (DEPTH, D_MODEL, IN_COLS)) * D_MODEL ** -0.5
    ssm_a_re = -0.5 + 0.01 * nrm(ks[3], (DEPTH, SSM_GROUPS, SSM_STATE))
    ssm_a_im = math.pi * jnp.arange(SSM_STATE, dtype=f32) + 0.01 * nrm(ks[4], (DEPTH, SSM_GROUPS, SSM_STATE))
    ssm_log_step = jax.random.uniform(ks[5], (DEPTH, SSM_GROUPS), dtype=f32, minval=math.log(DT_MIN), maxval=math.log(DT_MAX))
    ssm_b_re = nrm(ks[6], (DEPTH, SSM_GROUPS, SSM_STATE, SSM_GROUP)) * (2.0 * SSM_GROUP) ** -0.5
    ssm_b_im = nrm(ks[7], (DEPTH, SSM_GROUPS, SSM_STATE, SSM_GROUP)) * (2.0 * SSM_GROUP) ** -0.5
    ssm_c_re = nrm(ks[8], (DEPTH, SSM_GROUPS, SSM_GROUP, SSM_STATE)) * (2.0 * SSM_STATE) ** -0.5
    ssm_c_im = nrm(ks[9], (DEPTH, SSM_GROUPS, SSM_GROUP, SSM_STATE)) * (2.0 * SSM_STATE) ** -0.5
    ssm_d = nrm(ks[10], (DEPTH, SSM_WIDTH))
    ssm_w_glu = nrm(ks[11], (DEPTH, SSM_WIDTH, 2 * D_MODEL)) * SSM_WIDTH ** -0.5
    fox_forget_bias = 2.0 + 0.5 * nrm(ks[12], (DEPTH, ATT_HEADS))
    q_norm_g = 1.0 + 0.02 * nrm(ks[13], (DEPTH, HEAD_DIM))
    k_norm_g = 1.0 + 0.02 * nrm(ks[14], (DEPTH, HEAD_DIM))
    w_out = nrm(ks[15], (DEPTH, D_MODEL, D_MODEL)) * D_MODEL ** -0.5
    ffn_norm_g = 1.0 + 0.02 * nrm(ks[16], (DEPTH, D_MODEL))
    peer_w_query = nrm(ks[17], (DEPTH, D_MODEL, PEER_HEADS * PEER_QUERY)) * D_MODEL ** -0.5
    peer_sub_keys = nrm(ks[18], (DEPTH, PEER_HEADS, 2, PEER_KEYS, PEER_HALF)) * PEER_HALF ** -0.5
    peer_u = nrm(ks[19], (DEPTH, PEER_EXPERTS, D_MODEL)) * D_MODEL ** -0.5
    peer_v = nrm(ks[20], (DEPTH, PEER_EXPERTS, D_MODEL)) * 0.1
    return {"x": x, "mix_norm_g": mix_norm_g, "w_in": w_in,
            "ssm_a_re": ssm_a_re, "ssm_a_im": ssm_a_im, "ssm_log_step": ssm_log_step,
            "ssm_b_re": ssm_b_re, "ssm_b_im": ssm_b_im, "ssm_c_re": ssm_c_re, "ssm_c_im": ssm_c_im,
            "ssm_d": ssm_d, "ssm_w_glu": ssm_w_glu, "fox_forget_bias": fox_forget_bias,
            "q_norm_g": q_norm_g, "k_norm_g": k_norm_g, "w_out": w_out, "ffn_norm_g": ffn_norm_g,
            "peer_w_query": peer_w_query, "peer_sub_keys": peer_sub_keys,
            "peer_u": peer_u, "peer_v": peer_v}


def reference(x, mix_norm_g, w_in, ssm_a_re, ssm_a_im, ssm_log_step, ssm_b_re, ssm_b_im,
              ssm_c_re, ssm_c_im, ssm_d, ssm_w_glu, fox_forget_bias, q_norm_g, k_norm_g,
              w_out, ffn_norm_g, peer_w_query, peer_sub_keys, peer_u, peer_v):
    for layer in range(DEPTH):
        h = rms_norm(x, mix_norm_g[layer])
        proj = h @ w_in[layer]
        u, q, k, v, f_logit, g_ssm, g_att = jnp.split(proj, IN_SPLITS, axis=-1)
        y_ssm = s5_branch(u, ssm_a_re[layer], ssm_a_im[layer], ssm_log_step[layer],
                          ssm_b_re[layer], ssm_b_im[layer], ssm_c_re[layer], ssm_c_im[layer],
                          ssm_d[layer], ssm_w_glu[layer])
        y_att = forgetting_attention(q, k, v, f_logit, fox_forget_bias[layer],
                                     q_norm_g[layer], k_norm_g[layer])
        merged = jax.nn.sigmoid(g_ssm) * y_ssm + jax.nn.sigmoid(g_att) * y_att
        x = x + (merged @ w_out[layer]).astype(x.dtype)
        h2 = rms_norm(x, ffn_norm_g[layer])
        x = x + peer_ffn(h2, peer_w_query[layer], peer_sub_keys[layer],
                         peer_u[layer], peer_v[layer]).astype(x.dtype)
    return x
```

```python
import functools
import math

import numpy as np
import jax
import jax.numpy as jnp
from jax import lax
from jax.experimental import pallas as pl
from jax.experimental.pallas import tpu as pltpu

EPS = 1e-6
SSM_GROUP = 16
SSM_CHUNK = 16
HEAD_DIM = 128
PEER_TOPK = 16
VMEM_LIMIT_BYTES = 56 * 1024 * 1024

F32 = jnp.float32
BF16 = jnp.bfloat16
NEG_INF = float("-inf")


def _nt_dot(a, b):
    return lax.dot_general(a, b, (((1,), (1,)), ((), ())), preferred_element_type=F32)


def _params(*semantics):
    return pltpu.CompilerParams(dimension_semantics=semantics, vmem_limit_bytes=VMEM_LIMIT_BYTES)


def _inproj_kernel(x_ref, g_ref, wu_ref, wq_ref, wk_ref, wv_ref, wf_ref, wgs_ref, wga_ref,
                   qg_ref, kg_ref, fb_ref,
                   u_ref, q_ref, k_ref, v_ref, lf_ref, gs_ref, ga_ref, *, n_heads, q_scale):
    x = x_ref[...]
    h = x * lax.rsqrt(jnp.mean(x * x, axis=-1, keepdims=True) + EPS) * g_ref[...]
    hb = h.astype(BF16)
    u_ref[...] = jnp.dot(hb, wu_ref[...], preferred_element_type=F32).astype(BF16)

    def head_norm(w_ref, gain, o_ref, scale):
        y = jnp.dot(hb, w_ref[...], preferred_element_type=F32)
        for hd in range(n_heads):
            yh = y[:, hd * HEAD_DIM:(hd + 1) * HEAD_DIM]
            yn = yh * lax.rsqrt(jnp.mean(yh * yh, axis=-1, keepdims=True) + EPS) * gain
            if scale != 1.0:
                yn = yn * scale
            o_ref[:, hd * HEAD_DIM:(hd + 1) * HEAD_DIM] = yn.astype(BF16)

    head_norm(wq_ref, qg_ref[...], q_ref, q_scale)
    head_norm(wk_ref, kg_ref[...], k_ref, 1.0)
    v_ref[...] = jnp.dot(hb, wv_ref[...], preferred_element_type=F32).astype(BF16)
    f = jnp.dot(hb, wf_ref[...], preferred_element_type=F32) + fb_ref[...]
    lf_ref[...] = jax.nn.log_sigmoid(f)
    gs_ref[...] = jax.nn.sigmoid(jnp.dot(hb, wgs_ref[...], preferred_element_type=F32)).astype(BF16)
    ga_ref[...] = jax.nn.sigmoid(jnp.dot(hb, wga_ref[...], preferred_element_type=F32)).astype(BF16)


def _inproj(x2, g, wu, wq, wk, wv, wf, wgs, wga, qg, kg, fb, *, tm):
    n, d = x2.shape
    su, sa = wu.shape[1], wq.shape[1]
    row = lambda w: pl.BlockSpec((tm, w), lambda i: (i, 0))
    full = lambda a: pl.BlockSpec(a.shape, lambda i: (0,) * a.ndim)
    kern = functools.partial(_inproj_kernel, n_heads=sa // HEAD_DIM, q_scale=HEAD_DIM ** -0.5)
    return pl.pallas_call(
        kern,
        grid=(n // tm,),
        in_specs=[row(d), full(g), full(wu), full(wq), full(wk), full(wv), full(wf), full(wgs), full(wga),
                  full(qg), full(kg), full(fb)],
        out_specs=[row(su), row(sa), row(sa), row(sa), row(128), row(d), row(d)],
        out_shape=[jax.ShapeDtypeStruct((n, su), BF16), jax.ShapeDtypeStruct((n, sa), BF16),
                   jax.ShapeDtypeStruct((n, sa), BF16), jax.ShapeDtypeStruct((n, sa), BF16),
                   jax.ShapeDtypeStruct((n, 128), F32), jax.ShapeDtypeStruct((n, d), BF16),
                   jax.ShapeDtypeStruct((n, d), BF16)],
        compiler_params=_params("parallel"),
        name="inproj",
    )(x2, g, wu, wq, wk, wv, wf, wgs, wga, qg, kg, fb)


def _cumsum_kernel(lf_ref, o_ref, *, seq):
    y = lf_ref[0]
    lane = lax.broadcasted_iota(jnp.int32, y.shape, 1)
    s = 1
    while s < seq:
        y = y + jnp.where(lane >= s, pltpu.roll(y, s, axis=1), 0.0)
        s *= 2
    o_ref[0] = y


def _forget_cumsum(lf):
    b, h, l = lf.shape
    spec = pl.BlockSpec((1, h, l), lambda i: (i, 0, 0))
    return pl.pallas_call(
        functools.partial(_cumsum_kernel, seq=l),
        grid=(b,), in_specs=[spec], out_specs=spec,
        out_shape=jax.ShapeDtypeStruct((b, h, l), F32),
        compiler_params=_params("parallel"),
        name="forget_cumsum",
    )(lf)


def _s5_matrices(a_re, a_im, log_step, b_re, b_im, c_re, c_im, d_skip):
    t_len = SSM_CHUNK
    step = jnp.exp(log_step.astype(F32))[:, None]
    ar, ai = a_re.astype(F32), a_im.astype(F32)
    mag = jnp.exp(ar * step)
    lam_re = mag * jnp.cos(ai * step)
    lam_im = mag * jnp.sin(ai * step)
    nr, ni = lam_re - 1.0, lam_im
    den = ar * ar + ai * ai
    coef_re = (nr * ar + ni * ai) / den
    coef_im = (ni * ar - nr * ai) / den
    br = coef_re[..., None] * b_re.astype(F32) - coef_im[..., None] * b_im.astype(F32)
    bi = coef_re[..., None] * b_im.astype(F32) + coef_im[..., None] * b_re.astype(F32)
    cr, ci = c_re.astype(F32), c_im.astype(F32)
    g_n, p_n, h_n = br.shape
    dpow = jnp.arange(t_len + 1, dtype=F32)[:, None, None]
    pm = jnp.exp(ar[None] * step[None] * dpow)
    er = pm * jnp.cos(ai[None] * step[None] * dpow)
    ei = pm * jnp.sin(ai[None] * step[None] * dpow)
    lbr = er[..., None] * br[None] - ei[..., None] * bi[None]
    lbi = er[..., None] * bi[None] + ei[..., None] * br[None]
    kd = jnp.einsum('ghp,dgpj->dghj', cr, lbr) - jnp.einsum('ghp,dgpj->dghj', ci, lbi)
    lag = jnp.arange(t_len)[None, :] - jnp.arange(t_len)[:, None]
    blocks = jnp.where((lag >= 0)[:, :, None, None, None],
                       kd[jnp.clip(lag, 0, t_len)], 0.0)
    m = blocks.transpose(2, 0, 4, 1, 3).reshape(g_n, t_len * h_n, t_len * h_n)
    m = m + jax.vmap(jnp.diag)(jnp.tile(d_skip.astype(F32).reshape(g_n, 1, h_n), (1, t_len, 1)).reshape(g_n, -1))
    rev = (t_len - 1) - jnp.arange(t_len)
    p_re = lbr[rev].transpose(1, 0, 3, 2).reshape(g_n, t_len * h_n, p_n)
    p_im = lbi[rev].transpose(1, 0, 3, 2).reshape(g_n, t_len * h_n, p_n)
    p = jnp.concatenate([p_re, p_im], axis=-1)
    e1r, e1i = er[1:], ei[1:]
    q_re = cr[None] * e1r[:, :, None, :] - ci[None] * e1i[:, :, None, :]
    q_im = -cr[None] * e1i[:, :, None, :] - ci[None] * e1r[:, :, None, :]
    q = jnp.concatenate([q_re, q_im], axis=-1).transpose(1, 3, 0, 2).reshape(g_n, 2 * p_n, t_len * h_n)
    lam_a = jnp.concatenate([er[t_len], er[t_len]], axis=-1)
    lam_b = jnp.concatenate([-ei[t_len], ei[t_len]], axis=-1)
    return m.astype(BF16), p.astype(BF16), q.astype(BF16), lam_a, lam_b


def _s5_kernel(u_ref, m_ref, p_ref, q_ref, la_ref, lb_ref, y_ref, state_ref, pin_ref, sprev_ref,
               *, gb, n_chunks, batch, p2):
    @pl.when(pl.program_id(1) == 0)
    def _():
        state_ref[...] = jnp.zeros_like(state_ref)

    for g in range(gb):
        pin_ref[g] = jnp.dot(u_ref[g], p_ref[g], preferred_element_type=F32)

    la = la_ref[...]
    lb = lb_ref[...]

    def step(c, s):
        r0 = pl.multiple_of(c * batch, batch)
        sprev_ref[:, pl.ds(r0, batch), :] = s
        return la * s + lb * pltpu.roll(s, p2 // 2, axis=2) + pin_ref[:, pl.ds(r0, batch), :]

    state_ref[...] = lax.fori_loop(0, n_chunks, step, state_ref[...])

    for g in range(gb):
        y = jnp.dot(u_ref[g], m_ref[g], preferred_element_type=F32)
        y = y + jnp.dot(sprev_ref[g].astype(BF16), q_ref[g], preferred_element_type=F32)
        y_ref[g] = jax.nn.gelu(y).astype(BF16)


def _s5(u_g, m, p, q, lam_a, lam_b, *, batch, gb, rb):
    g_n, rows, width = u_g.shape
    p2 = p.shape[-1]
    blk = lambda w: pl.BlockSpec((gb, rb, w), lambda gi, ri: (gi, ri, 0))
    wspec = lambda a: pl.BlockSpec((gb,) + a.shape[1:], lambda gi, ri: (gi, 0, 0))
    kern = functools.partial(_s5_kernel, gb=gb, n_chunks=rb // batch, batch=batch, p2=p2)
    return pl.pallas_call(
        kern,
        grid=(g_n // gb, rows // rb),
        in_specs=[blk(width), wspec(m), wspec(p), wspec(q), wspec(lam_a), wspec(lam_b)],
        out_specs=blk(width),
        out_shape=jax.ShapeDtypeStruct((g_n, rows, width), BF16),
        scratch_shapes=[pltpu.VMEM((gb, batch, p2), F32), pltpu.VMEM((gb, rb, p2), F32),
                        pltpu.VMEM((gb, rb, p2), F32)],
        compiler_params=_params("parallel", "arbitrary"),
        name="s5_scan",
    )(u_g, m, p, q, lam_a, lam_b)


def _attn_kernel(qi_ref, kj_ref, q_ref, k_ref, v_ref, ck_ref, cq_ref, o_ref, m_ref, l_ref, acc_ref, *, tq, tk):
    step = pl.program_id(2)
    qi = qi_ref[step]
    kj = kj_ref[step]

    @pl.when(kj == 0)
    def _():
        m_ref[...] = jnp.full_like(m_ref, NEG_INF)
        l_ref[...] = jnp.zeros_like(l_ref)
        acc_ref[...] = jnp.zeros_like(acc_ref)

    bias = cq_ref[0, 0, :, 0:1] - ck_ref[0, 0]
    z = _nt_dot(q_ref[0], k_ref[0]) + bias
    q_pos = qi * tq + lax.broadcasted_iota(jnp.int32, (tq, tk), 0)
    k_pos = kj * tk + lax.broadcasted_iota(jnp.int32, (tq, tk), 1)
    z = jnp.where(k_pos <= q_pos, z, NEG_INF)
    m_old = m_ref[...]
    m_new = jnp.maximum(m_old, jnp.max(z, axis=-1, keepdims=True))
    alpha = jnp.exp(m_old - m_new)
    p = jnp.exp(z - m_new)
    l_ref[...] = alpha * l_ref[...] + jnp.sum(p, axis=-1, keepdims=True)
    acc_ref[...] = alpha * acc_ref[...] + jnp.dot(p.astype(BF16), v_ref[0], preferred_element_type=F32)
    m_ref[...] = m_new

    @pl.when((kj + 1) * tk >= (qi + 1) * tq)
    def _():
        o_ref[0] = (acc_ref[...] / l_ref[...]).astype(BF16)


def _attention(q, k, v, cum, *, tq, tk):
    b, l, w = q.shape
    n_heads = w // HEAD_DIM
    pairs = [(i, j) for i in range(l // tq) for j in range(((i + 1) * tq + tk - 1) // tk)]
    qi = jnp.asarray(np.array([p[0] for p in pairs], np.int32))
    kj = jnp.asarray(np.array([p[1] for p in pairs], np.int32))
    grid_spec = pltpu.PrefetchScalarGridSpec(
        num_scalar_prefetch=2,
        grid=(b, n_heads, len(pairs)),
        in_specs=[
            pl.BlockSpec((1, tq, HEAD_DIM), lambda bi, hi, s, qi, kj: (bi, qi[s], hi)),
            pl.BlockSpec((1, tk, HEAD_DIM), lambda bi, hi, s, qi, kj: (bi, kj[s], hi)),
            pl.BlockSpec((1, tk, HEAD_DIM), lambda bi, hi, s, qi, kj: (bi, kj[s], hi)),
            pl.BlockSpec((1, 1, 1, tk), lambda bi, hi, s, qi, kj: (bi, hi, 0, kj[s])),
            pl.BlockSpec((1, 1, 1, tq), lambda bi, hi, s, qi, kj: (bi, hi, 0, qi[s])),
        ],
        out_specs=pl.BlockSpec((1, tq, HEAD_DIM), lambda bi, hi, s, qi, kj: (bi, qi[s], hi)),
        scratch_shapes=[pltpu.VMEM((tq, 1), F32), pltpu.VMEM((tq, 1), F32), pltpu.VMEM((tq, HEAD_DIM), F32)],
    )
    return pl.pallas_call(
        functools.partial(_attn_kernel, tq=tq, tk=tk),
        grid_spec=grid_spec,
        out_shape=jax.ShapeDtypeStruct((b, l, w), BF16),
        compiler_params=_params("parallel", "parallel", "arbitrary"),
        name="fox_attention",
    )(qi, kj, q, k, v, cum, cum)


def _merge_kernel(yg_ref, att_ref, gs_ref, ga_ref, x_ref, wglu_ref, wout_ref, o_ref, *, d):
    z = jnp.dot(yg_ref[...], wglu_ref[...], preferred_element_type=F32)
    y_ssm = z[:, :d] * jax.nn.sigmoid(z[:, d:])
    merged = gs_ref[...].astype(F32) * y_ssm + ga_ref[...].astype(F32) * att_ref[...].astype(F32)
    o_ref[...] = x_ref[...] + jnp.dot(merged.astype(BF16), wout_ref[...], preferred_element_type=F32)


def _merge(yg, att, gs, ga, x2, wglu, wout, *, tm):
    n, d = x2.shape
    row = lambda w: pl.BlockSpec((tm, w), lambda i: (i, 0))
    full = lambda a: pl.BlockSpec(a.shape, lambda i: (0,) * a.ndim)
    return pl.pallas_call(
        functools.partial(_merge_kernel, d=d),
        grid=(n // tm,),
        in_specs=[row(yg.shape[1]), row(att.shape[1]), row(d), row(d), row(d), full(wglu), full(wout)],
        out_specs=row(d),
        out_shape=jax.ShapeDtypeStruct((n, d), F32),
        compiler_params=_params("parallel"),
        name="merge_outproj",
    )(yg, att, gs, ga, x2, wglu, wout)


def _top_values(s, rows_ref, k):
    idx = lax.broadcasted_iota(jnp.int32, s.shape, 0)
    for r in range(k):
        m = jnp.max(s, axis=0, keepdims=True)
        rows_ref[r:r + 1, :] = m
        if r + 1 < k:
            first = jnp.min(jnp.where(s == m, idx, s.shape[0]), axis=0, keepdims=True)
            s = jnp.where(idx == first, NEG_INF, s)


def _peer_kernel(x_ref, g_ref, wq_ref, keys_ref, u_ref, vt_ref, o_ref,
                 h2_ref, s2_ref, e2_ref, d_ref, c_ref, a_ref, b_ref, cand_ref, w_ref, acc_ref,
                 *, n_heads, n_keys, eb, cand_pairs):
    j = pl.program_id(1)
    n_il = eb // n_keys
    t_len = x_ref.shape[0]

    @pl.when(j == 0)
    def _():
        x = x_ref[...]
        h2 = (x * lax.rsqrt(jnp.mean(x * x, axis=-1, keepdims=True) + EPS) * g_ref[...]).astype(BF16)
        h2_ref[...] = h2
        acc_ref[...] = jnp.zeros_like(acc_ref)
        cand_ref[...] = jnp.full_like(cand_ref, NEG_INF)

        def head(h, carry):
            qh = jnp.dot(h2, wq_ref[h], preferred_element_type=F32).astype(BF16)
            half = qh.shape[1] // 2
            s1 = _nt_dot(keys_ref[h, 0], qh[:, :half])
            s2 = _nt_dot(keys_ref[h, 1], qh[:, half:])
            _top_values(s1, a_ref, PEER_TOPK)
            _top_values(s2, b_ref, PEER_TOPK)
            off = 0
            for i, cnt in cand_pairs:
                cand_ref[off:off + cnt, :] = a_ref[i:i + 1, :] + b_ref[0:cnt, :]
                off += cnt
            cand = cand_ref[...]
            _top_values(cand, a_ref, PEER_TOPK)
            tau = a_ref[PEER_TOPK - 1:PEER_TOPK, :]
            a0 = jnp.max(s1, axis=0, keepdims=True)
            b0 = jnp.max(s2, axis=0, keepdims=True)
            zsum = jnp.sum(jnp.where(cand >= tau, jnp.exp(cand - (a0 + b0)), 0.0), axis=0, keepdims=True)
            cut = jnp.full_like(s1, float("inf"))
            for r in range(PEER_TOPK):
                b_r = b_ref[r:r + 1, :]
                cut = jnp.where(s1 + b_r >= tau, b_r, cut)
            d_ref[h] = cut.reshape(n_keys // 8, 8, t_len)
            c_ref[h] = (jnp.exp(s1 - a0) / zsum).reshape(n_keys // 8, 8, t_len)
            s2_ref[h] = s2
            e2_ref[h] = jnp.exp(s2 - b0)
            return carry

        lax.fori_loop(0, n_heads, head, 0)

    act_t = _nt_dot(u_ref[...], h2_ref[...])
    for il in range(n_il):
        tile = j * (n_il // 8) + il // 8
        gate = jnp.zeros((n_keys, t_len), F32)
        for h in range(n_heads):
            d_row = d_ref[h, tile, il % 8:il % 8 + 1, :]
            c_row = c_ref[h, tile, il % 8:il % 8 + 1, :]
            gate = gate + jnp.where(s2_ref[h] >= d_row, e2_ref[h], 0.0) * c_row
        a_il = act_t[il * n_keys:(il + 1) * n_keys, :]
        w_ref[il * n_keys:(il + 1) * n_keys, :] = (gate * jax.nn.gelu(a_il)).astype(BF16)
    acc_ref[...] += jnp.dot(vt_ref[...], w_ref[...], preferred_element_type=F32)

    @pl.when(j == pl.num_programs(1) - 1)
    def _():
        o_ref[...] = x_ref[...] + acc_ref[...].T


def _peer(x1, g, wq_h, keys, u_tab, vt_tab, *, tt, eb):
    n, d = x1.shape
    n_heads, _, n_keys, half = keys.shape
    n_exp = u_tab.shape[0]
    cand_pairs = tuple((i, PEER_TOPK // (i + 1)) for i in range(PEER_TOPK))
    n_cand = sum(c for _, c in cand_pairs)
    cand_rows = -(-n_cand // 8) * 8
    kern = functools.partial(_peer_kernel, n_heads=n_heads, n_keys=n_keys, eb=eb, cand_pairs=cand_pairs)
    return pl.pallas_call(
        kern,
        grid=(n // tt, n_exp // eb),
        in_specs=[
            pl.BlockSpec((tt, d), lambda i, j: (i, 0)),
            pl.BlockSpec(g.shape, lambda i, j: (0, 0)),
            pl.BlockSpec(wq_h.shape, lambda i, j: (0, 0, 0)),
            pl.BlockSpec(keys.shape, lambda i, j: (0, 0, 0, 0)),
            pl.BlockSpec((eb, d), lambda i, j: (j, 0)),
            pl.BlockSpec((d, eb), lambda i, j: (0, j)),
        ],
        out_specs=pl.BlockSpec((tt, d), lambda i, j: (i, 0)),
        out_shape=jax.ShapeDtypeStruct((n, d), F32),
        scratch_shapes=[
            pltpu.VMEM((tt, d), BF16),
            pltpu.VMEM((n_heads, n_keys, tt), F32),
            pltpu.VMEM((n_heads, n_keys, tt), F32),
            pltpu.VMEM((n_heads, n_keys // 8, 8, tt), F32),
            pltpu.VMEM((n_heads, n_keys // 8, 8, tt), F32),
            pltpu.VMEM((PEER_TOPK, tt), F32),
            pltpu.VMEM((PEER_TOPK, tt), F32),
            pltpu.VMEM((cand_rows, tt), F32),
            pltpu.VMEM((eb, tt), BF16),
            pltpu.VMEM((d, tt), F32),
        ],
        compiler_params=_params("parallel", "arbitrary"),
        name="peer_dense",
    )(x1, g, wq_h, keys, u_tab, vt_tab)


def _layer(x, mix_norm_g, w_in, ssm_a_re, ssm_a_im, ssm_log_step, ssm_b_re, ssm_b_im, ssm_c_re, ssm_c_im,
           ssm_d, ssm_w_glu, fox_forget_bias, q_norm_g, k_norm_g, w_out, ffn_norm_g,
           peer_w_query, peer_sub_keys, peer_u, peer_v):
    bsz, seq, d = x.shape
    n = bsz * seq
    ssm_w = ssm_d.shape[0]
    n_groups = ssm_w // SSM_GROUP
    n_heads = fox_forget_bias.shape[0]
    att_w = n_heads * HEAD_DIM
    x2 = x.reshape(n, d)

    o = 0
    cols = []
    for wdt in (ssm_w, att_w, att_w, att_w, n_heads, d, d):
        cols.append(w_in[:, o:o + wdt].astype(BF16))
        o += wdt
    wu, wq, wk, wv, wf, wgs, wga = cols
    wf = jnp.pad(wf, ((0, 0), (0, 128 - n_heads)))
    fb = jnp.pad(fox_forget_bias.astype(F32), (0, 128 - n_heads)).reshape(1, 128)

    tm = min(512, n)
    u, q, k, v, lf, gs, ga = _inproj(
        x2, mix_norm_g.reshape(1, d).astype(F32), wu, wq, wk, wv, wf, wgs, wga,
        q_norm_g.reshape(1, HEAD_DIM).astype(F32), k_norm_g.reshape(1, HEAD_DIM).astype(F32), fb, tm=tm)

    lf_t = lf[:, :n_heads].reshape(bsz, seq, n_heads).transpose(0, 2, 1)
    cum = _forget_cumsum(lf_t).reshape(bsz, n_heads, 1, seq)

    n_chunks = seq // SSM_CHUNK
    u_g = (u.reshape(bsz, n_chunks, SSM_CHUNK, n_groups, SSM_GROUP)
           .transpose(3, 1, 0, 2, 4).reshape(n_groups, n_chunks * bsz, SSM_CHUNK * SSM_GROUP))
    m, p, qm, lam_a, lam_b = _s5_matrices(ssm_a_re, ssm_a_im, ssm_log_step, ssm_b_re, ssm_b_im,
                                          ssm_c_re, ssm_c_im, ssm_d)
    rb = min(512, n_chunks * bsz)
    y_g = _s5(u_g, m, p, qm, lam_a[:, None, :], lam_b[:, None, :], batch=bsz, gb=min(8, n_groups), rb=rb)
    yg = (y_g.reshape(n_groups, n_chunks, bsz, SSM_CHUNK, SSM_GROUP)
          .transpose(2, 1, 3, 0, 4).reshape(n, ssm_w))

    tq = min(1024, seq)
    tk = min(512, seq)
    att = _attention(q.reshape(bsz, seq, att_w), k.reshape(bsz, seq, att_w), v.reshape(bsz, seq, att_w), cum,
                     tq=tq, tk=tk).reshape(n, att_w)

    x1 = _merge(yg, att, gs, ga, x2, ssm_w_glu.astype(BF16), w_out.astype(BF16), tm=tm)

    p_heads, _, n_keys, half = peer_sub_keys.shape
    wq_h = peer_w_query.astype(BF16).reshape(d, p_heads, 2 * half).transpose(1, 0, 2)
    out = _peer(x1, ffn_norm_g.reshape(1, d).astype(F32), wq_h, peer_sub_keys.astype(BF16),
                peer_u.astype(BF16), peer_v.astype(BF16).T, tt=min(512, n), eb=1024)
    return out.reshape(bsz, seq, d)


def kernel(x, mix_norm_g, w_in, ssm_a_re, ssm_a_im, ssm_log_step, ssm_b_re, ssm_b_im, ssm_c_re, ssm_c_im, ssm_d, ssm_w_glu, fox_forget_bias, q_norm_g, k_norm_g, w_out, ffn_norm_g, peer_w_query, peer_sub_keys, peer_u, peer_v):
    for layer in range(mix_norm_g.shape[0]):
        x = _layer(x, mix_norm_g[layer], w_in[layer], ssm_a_re[layer], ssm_a_im[layer], ssm_log_step[layer],
                   ssm_b_re[layer], ssm_b_im[layer], ssm_c_re[layer], ssm_c_im[layer], ssm_d[layer],
                   ssm_w_glu[layer], fox_forget_bias[layer], q_norm_g[layer], k_norm_g[layer], w_out[layer],
                   ffn_norm_g[layer], peer_w_query[layer], peer_sub_keys[layer], peer_u[layer], peer_v[layer])
    return x
```
